```python
import jax, jax.numpy as jnp
from jax import lax
import numpy as np

D_MODEL = 2048
BATCH = 2
SEQ = 4096
DEPTH = 1
DEC_BATCH = 32
DEC_SEQ = 1
PAST_LEN = 8192
PAGE_SIZE = 128

MIX_WIDTH = D_MODEL
C_CONV = MIX_WIDTH // 2
N_HEADS = 8
HEAD_DIM = (MIX_WIDTH - C_CONV) // N_HEADS
ATTN_W = N_HEADS * HEAD_DIM
CONV_W = 31
N_IDX_HEADS = 16
D_IDX = 64
TOPK_MAX = 256
D_FF = 4 * D_MODEL
ROPE_THETA = 10000.0
EPS = 1e-6
Q_BLOCK = 128

IN_SPLITS = (C_CONV, 2 * C_CONV, 2 * C_CONV + ATTN_W, 2 * C_CONV + 2 * ATTN_W,
             2 * C_CONV + 3 * ATTN_W, 2 * C_CONV + 3 * ATTN_W + N_IDX_HEADS * D_IDX,
             2 * C_CONV + 3 * ATTN_W + N_IDX_HEADS * D_IDX + D_IDX)
D_IN = 2 * C_CONV + 3 * ATTN_W + N_IDX_HEADS * D_IDX + D_IDX + N_IDX_HEADS

kernel_name = 'hymba_conformer_dsa_step'

F32 = jnp.float32


def rms_norm(x, g):
    xf = x.astype(F32)
    y = xf * lax.rsqrt(jnp.mean(xf * xf, axis=-1, keepdims=True) + EPS)
    return (y * g.astype(F32)).astype(x.dtype)


def layer_norm(x, g, b):
    xf = x.astype(F32)
    mu = jnp.mean(xf, axis=-1, keepdims=True)
    var = jnp.mean(jnp.square(xf - mu), axis=-1, keepdims=True)
    return ((xf - mu) * lax.rsqrt(var + EPS) * g.astype(F32) + b.astype(F32)).astype(x.dtype)


def rope(x, pos):
    half = x.shape[-1] // 2
    inv_freq = ROPE_THETA ** (-jnp.arange(half, dtype=F32) / half)
    ang = pos.astype(F32)[:, None] * inv_freq[None, :]
    cos = jnp.cos(ang)[None, :, None, :]
    sin = jnp.sin(ang)[None, :, None, :]
    xf = x.astype(F32)
    x1, x2 = xf[..., :half], xf[..., half:]
    return jnp.concatenate([x1 * cos - x2 * sin, x2 * cos + x1 * sin], axis=-1).astype(x.dtype)


def causal_depthwise_conv(u_ext, conv_w, conv_b):
    y = lax.conv_general_dilated(u_ext, conv_w[:, None, :].astype(u_ext.dtype), window_strides=(1,),
                                 padding='VALID', dimension_numbers=('NWC', 'WIO', 'NWC'),
                                 feature_group_count=u_ext.shape[-1])
    return y + conv_b.astype(y.dtype)


def index_scores(qi, ki, wi):
    dots = jnp.einsum('nqhd,nsd->nqhs', qi, ki).astype(F32) * (D_IDX ** -0.5)
    return jnp.einsum('nqh,nqhs->nqs', wi.astype(F32) * (N_IDX_HEADS ** -0.5), jax.nn.relu(dots))


def attend_gathered(q, kg, vg, ok):
    s = jnp.einsum('nqhd,nqkhd->nqhk', q, kg).astype(F32) * (HEAD_DIM ** -0.5)
    s = jnp.where(ok[:, :, None, :], s, -jnp.inf)
    p = jax.nn.softmax(s, axis=-1).astype(vg.dtype)
    return jnp.einsum('nqhk,nqkhd->nqhd', p, vg)


def sparse_attend_prompt(q, k, v, qi, ki, wi):
    nb, t = q.shape[:2]
    topk = min(TOPK_MAX, t // 4)
    spos = jnp.arange(t)
    bidx = jnp.arange(nb)[:, None, None]

    def block(start):
        qb = lax.dynamic_slice_in_dim(q, start, Q_BLOCK, axis=1)
        qib = lax.dynamic_slice_in_dim(qi, start, Q_BLOCK, axis=1)
        wib = lax.dynamic_slice_in_dim(wi, start, Q_BLOCK, axis=1)
        tpos = start + jnp.arange(Q_BLOCK)
        allowed = spos[None, :] <= tpos[:, None]
        scores = jnp.where(allowed[None], index_scores(qib, ki, wib), -jnp.inf)
        _, idx = lax.top_k(scores, topk)
        ok = idx <= tpos[None, :, None]
        return attend_gathered(qb, k[bidx, idx], v[bidx, idx], ok)

    out = lax.map(block, jnp.arange(0, t, Q_BLOCK))
    return jnp.moveaxis(out, 0, 1).reshape(nb, t, N_HEADS, HEAD_DIM)


def sparse_attend_sample(q, k, v, qi, ki, wi, cache_k, cache_v, cache_kidx, page_table, layer):
    nb, t_new = q.shape[:2]
    past = page_table.shape[1] * PAGE_SIZE
    n_keys = past + t_new
    topk = min(TOPK_MAX, n_keys // 4)
    ki_past = cache_kidx[layer, page_table].reshape(nb, past, D_IDX).astype(ki.dtype)
    ki_all = jnp.concatenate([ki_past, ki], axis=1)
    tpos = past + jnp.arange(t_new)
    allowed = jnp.arange(n_keys)[None, :] <= tpos[:, None]
    scores = jnp.where(allowed[None], index_scores(qi, ki_all, wi), -jnp.inf)
    _, idx = lax.top_k(scores, topk)
    ok = idx <= tpos[None, :, None]
    bidx = jnp.arange(nb)[:, None, None]
    pidx = jnp.minimum(idx, past - 1)
    phys = page_table[bidx, pidx // PAGE_SIZE]
    off = pidx % PAGE_SIZE
    nidx = jnp.clip(idx - past, 0, t_new - 1)
    from_past = (idx < past)[..., None, None]
    kg = jnp.where(from_past, cache_k[layer, phys, off].astype(k.dtype), k[bidx, nidx])
    vg = jnp.where(from_past, cache_v[layer, phys, off].astype(v.dtype), v[bidx, nidx])
    return attend_gathered(q, kg, vg, ok)


def hybrid_layer(x, c, pos, conv_prev, attend_fn, norm1_g, w_ada, b_ada, w_in, conv_w, conv_b,
                 conv_ln_g, conv_ln_b, w_out, norm2_g, w_ff1, w_ff2):
    nb, t, _ = x.shape
    mod = (jax.nn.silu(c) @ w_ada + b_ada)[:, None, :]
    shift1, scale1, gate1, shift2, scale2, gate2 = jnp.split(mod, 6, axis=-1)
    h = rms_norm(x, norm1_g) * (1 + scale1) + shift1
    u_a, u_g, q, k, v, qi, ki, wi = jnp.split(h @ w_in, IN_SPLITS, axis=-1)
    u = u_a * jax.nn.sigmoid(u_g)
    u_ext = jnp.concatenate([conv_prev.astype(u.dtype), u], axis=1)
    a_out = jax.nn.silu(layer_norm(causal_depthwise_conv(u_ext, conv_w, conv_b), conv_ln_g, conv_ln_b))
    q = rope(q.reshape(nb, t, N_HEADS, HEAD_DIM), pos)
    k = rope(k.reshape(nb, t, N_HEADS, HEAD_DIM), pos)
    v = v.reshape(nb, t, N_HEADS, HEAD_DIM)
    qi = rope(qi.reshape(nb, t, N_IDX_HEADS, D_IDX), pos)
    ki = rope(ki[:, :, None, :], pos)[:, :, 0, :]
    b_out = attend_fn(q, k, v, qi, ki, wi).reshape(nb, t, ATTN_W)
    x = x + gate1 * (jnp.concatenate([a_out, b_out], axis=-1) @ w_out)
    h2 = rms_norm(x, norm2_g) * (1 + scale2) + shift2
    x = x + gate2 * (jnp.square(jax.nn.relu(h2 @ w_ff1)) @ w_ff2)
    return x, k, v, ki, u_ext[:, -(CONV_W - 1):]


def setup_inputs(seed: int = 0) -> dict:
    key = jax.random.key(seed)
    ks = iter(jax.random.split(key, 32))

    def nrm(shape, scale):
        return jax.random.normal(next(ks), shape, F32) * scale

    n_pages = PAST_LEN // PAGE_SIZE
    n_used = DEC_BATCH * n_pages
    n_pool = n_used + n_used // 4
    x_prompt = nrm((BATCH, SEQ, D_MODEL), 1.0)
    x_sample = nrm((DEC_BATCH, DEC_SEQ, D_MODEL), 1.0)
    cache_k = nrm((DEPTH, n_pool, PAGE_SIZE, N_HEADS, HEAD_DIM), 1.0)
    cache_v = nrm((DEPTH, n_pool, PAGE_SIZE, N_HEADS, HEAD_DIM), 1.0)
    cache_kidx = nrm((DEPTH, n_pool, PAGE_SIZE, D_IDX), 1.0)
    state_conv = nrm((DEPTH, DEC_BATCH, CONV_W - 1, C_CONV), 0.5)
    page_table = jax.random.permutation(next(ks), n_pool)[:n_used].reshape(DEC_BATCH, n_pages).astype(jnp.int32)
    c_prompt = nrm((BATCH, D_MODEL), 1.0)
    c_sample = nrm((DEC_BATCH, D_MODEL), 1.0)
    return {
        'x_prompt': x_prompt,
        'x_sample': x_sample,
        'cache_k': cache_k,
        'cache_v': cache_v,
        'cache_kidx': cache_kidx,
        'state_conv': state_conv,
        'page_table': page_table,
        'c_prompt': c_prompt,
        'c_sample': c_sample,
        'norm1_g': 1.0 + nrm((DEPTH, D_MODEL), 0.02),
        'w_ada': nrm((DEPTH, D_MODEL, 6 * D_MODEL), D_MODEL ** -0.5),
        'b_ada': nrm((DEPTH, 6 * D_MODEL), 0.02),
        'w_in': nrm((DEPTH, D_MODEL, D_IN), D_MODEL ** -0.5),
        'conv_w': nrm((DEPTH, CONV_W, C_CONV), CONV_W ** -0.5),
        'conv_b': nrm((DEPTH, C_CONV), 0.02),
        'conv_ln_g': 1.0 + nrm((DEPTH, C_CONV), 0.02),
        'conv_ln_b': nrm((DEPTH, C_CONV), 0.02),
        'w_out': nrm((DEPTH, MIX_WIDTH, D_MODEL), MIX_WIDTH ** -0.5),
        'norm2_g': 1.0 + nrm((DEPTH, D_MODEL), 0.02),
        'w_ff1': nrm((DEPTH, D_MODEL, D_FF), D_MODEL ** -0.5),
        'w_ff2': nrm((DEPTH, D_FF, D_MODEL), D_FF ** -0.5),
        'final_g': 1.0 + nrm((D_MODEL,), 0.02),
    }


def reference(x_prompt, x_sample, cache_k, cache_v, cache_kidx, state_conv, page_table, c_prompt, c_sample,
              norm1_g, w_ada, b_ada, w_in, conv_w, conv_b, conv_ln_g, conv_ln_b, w_out, norm2_g,
              w_ff1, w_ff2, final_g):
    past = page_table.shape[1] * PAGE_SIZE
    pos_p = jnp.arange(x_prompt.shape[1])
    pos_s = past + jnp.arange(x_sample.shape[1])
    xp, xs = x_prompt, x_sample
    kp, vp, kip, cp, ksl, vsl, kis, cs = [], [], [], [], [], [], [], []
    for l in range(DEPTH):
        w = (norm1_g[l], w_ada[l], b_ada[l], w_in[l], conv_w[l], conv_b[l], conv_ln_g[l], conv_ln_b[l],
             w_out[l], norm2_g[l], w_ff1[l], w_ff2[l])
        conv0 = jnp.zeros((xp.shape[0], CONV_W - 1, C_CONV), xp.dtype)
        xp, k1, v1, ki1, c1 = hybrid_layer(xp, c_prompt, pos_p, conv0, sparse_attend_prompt, *w)

        def attend_s(q, k, v, qi, ki, wi, layer=l):
            return sparse_attend_sample(q, k, v, qi, ki, wi, cache_k, cache_v, cache_kidx, page_table, layer)

        xs, k2, v2, ki2, c2 = hybrid_layer(xs, c_sample, pos_s, state_conv[l], attend_s, *w)
        kp.append(k1); vp.append(v1); kip.append(ki1); cp.append(c1)
        ksl.append(k2); vsl.append(v2); kis.append(ki2); cs.append(c2)
    y_prompt = rms_norm(xp, final_g)
    y_sample = rms_norm(xs, final_g)
    k_prompt, v_prompt, kidx_prompt, conv_prompt = jnp.stack(kp), jnp.stack(vp), jnp.stack(kip), jnp.stack(cp)
    k_sample, v_sample, kidx_sample, conv_sample = jnp.stack(ksl), jnp.stack(vsl), jnp.stack(kis), jnp.stack(cs)
    return (y_prompt, y_sample, k_prompt, v_prompt, kidx_prompt, conv_prompt,
            k_sample, v_sample, kidx_sample, conv_sample)
```

```python
import functools

import jax
import jax.numpy as jnp
from jax import lax
from jax.experimental import pallas as pl
from jax.experimental.pallas import tpu as pltpu

F32 = jnp.float32
BF16 = jnp.bfloat16
I32 = jnp.int32

N_HEADS = 8
HEAD_DIM = 128
N_IDX_HEADS = 16
D_IDX = 64
CONV_W = 31
TOPK_MAX = 256
PAGE_SIZE = 128
PAGE_SHIFT = PAGE_SIZE.bit_length() - 1
Q_BLOCK = 128
ROPE_THETA = 10000.0
EPS = 1e-6

LANES = 128
SUBLANES = 8
VMEM_LIMIT = 56 * 1024 * 1024
MASKED_SCORE = -1e30
RUNNING_MAX_INIT = -1e20
FLT_MAX = 3.4028234663852886e38
INT_MIN = -2147483648


def _params(*sem):
    return pltpu.CompilerParams(dimension_semantics=sem, vmem_limit_bytes=VMEM_LIMIT)


def _dot(a, b):
    return jnp.dot(a, b, preferred_element_type=F32)


def _dot_nt(a, b):
    return lax.dot_general(a, b, (((1,), (1,)), ((), ())), preferred_element_type=F32)


def _ordered_to_float(u):
    key = u ^ jnp.int32(INT_MIN)
    bits = key ^ ((key >> 31) & jnp.int32(0x7FFFFFFF))
    return pltpu.bitcast(bits, F32)


def _kth_largest(count_ge, topk, shape):
    def step(i, res):
        cand = res | lax.shift_left(jnp.int32(1), 31 - i)
        return jnp.where(count_ge(_ordered_to_float(cand)) >= topk, cand, res)

    res = lax.fori_loop(0, 32, step, jnp.zeros(shape, I32))
    finite = lax.shift_right_logical(res, jnp.full(shape, 23, I32)) != 0
    return jnp.where(finite, _ordered_to_float(res), -FLT_MAX)


def _ada_kernel(c_ref, w_ref, b_ref, o_ref):
    c = c_ref[...]
    a = (c * jax.nn.sigmoid(c)).astype(BF16)
    o_ref[...] = _dot(a, w_ref[...].astype(BF16)) + b_ref[...]


def _ada(c_all, w_ada, b_ada):
    r, d = c_all.shape
    n = w_ada.shape[1]
    tn = 1024
    return pl.pallas_call(
        _ada_kernel,
        grid=(n // tn,),
        in_specs=[pl.BlockSpec((r, d), lambda j: (0, 0)),
                  pl.BlockSpec((d, tn), lambda j: (0, j)),
                  pl.BlockSpec((1, tn), lambda j: (0, j))],
        out_specs=pl.BlockSpec((r, tn), lambda j: (0, j)),
        out_shape=jax.ShapeDtypeStruct((r, n), F32),
        compiler_params=_params("arbitrary"),
        name="ada",
    )(c_all, w_ada, b_ada.reshape(1, n))


def _mod_spec(per_row, tm, tiles_per_batch, d):
    if per_row:
        return pl.BlockSpec((tm, d), lambda i, *_: (i, 0))
    return pl.BlockSpec((None, 1, d), lambda i, *_: (i // tiles_per_batch, 0, 0))


def _prenorm_kernel(x_ref, g_ref, shift_ref, scale_ref, o_ref):
    x = x_ref[...]
    y = x * lax.rsqrt(jnp.mean(x * x, axis=-1, keepdims=True) + EPS) * g_ref[...]
    o_ref[...] = (y * (1.0 + scale_ref[...]) + shift_ref[...]).astype(o_ref.dtype)


def _prenorm(x, g, shift, scale, per_row, tm, tiles_per_batch):
    m, d = x.shape
    ms = _mod_spec(per_row, tm, tiles_per_batch, d)
    return pl.pallas_call(
        _prenorm_kernel,
        grid=(m // tm,),
        in_specs=[pl.BlockSpec((tm, d), lambda i: (i, 0)),
                  pl.BlockSpec((1, d), lambda i: (0, 0)), ms, ms],
        out_specs=pl.BlockSpec((tm, d), lambda i: (i, 0)),
        out_shape=jax.ShapeDtypeStruct((m, d), BF16),
        compiler_params=_params("parallel"),
        name="prenorm",
    )(x, g.reshape(1, d), shift, scale)


def _rope_tables(pos, head_w):
    half = head_w // 2
    inv_freq = ROPE_THETA ** (-jnp.arange(half, dtype=F32) / half)
    ang = pos.astype(F32)[:, None] * inv_freq[None, :]
    cos, sin = jnp.cos(ang), jnp.sin(ang)
    reps = LANES // head_w
    cos_t = jnp.tile(jnp.concatenate([cos, cos], axis=-1), (1, reps))
    sin_t = jnp.tile(jnp.concatenate([-sin, sin], axis=-1), (1, reps))
    return cos_t, sin_t


def _rope_block(x, cos, sin, head_w):
    if head_w == LANES:
        r = pltpu.roll(x, LANES // 2, 1)
    else:
        lane = lax.broadcasted_iota(I32, x.shape, 1)
        lower = (lane % head_w) < (head_w // 2)
        r = jnp.where(lower, pltpu.roll(x, LANES - head_w // 2, 1), pltpu.roll(x, head_w // 2, 1))
    return x * cos + r * sin


def _glu_kernel(h_ref, wa_ref, wg_ref, u_ref, wa_sc, wg_sc):
    @pl.when(pl.program_id(1) == 0)
    def _():
        wa_sc[...] = wa_ref[...].astype(BF16)
        wg_sc[...] = wg_ref[...].astype(BF16)

    h = h_ref[...]
    a = _dot(h, wa_sc[...])
    g = _dot(h, wg_sc[...])
    u_ref[...] = a * jax.nn.sigmoid(g)


def _proj_glu(h, w_in, c_conv, tm):
    m, d = h.shape
    tn = 512
    nc = c_conv // tn
    return pl.pallas_call(
        _glu_kernel,
        grid=(nc, m // tm),
        in_specs=[pl.BlockSpec((tm, d), lambda c, i: (i, 0)),
                  pl.BlockSpec((d, tn), lambda c, i: (0, c)),
                  pl.BlockSpec((d, tn), lambda c, i: (0, nc + c))],
        out_specs=pl.BlockSpec((tm, tn), lambda c, i: (i, c)),
        out_shape=jax.ShapeDtypeStruct((m, c_conv), F32),
        scratch_shapes=[pltpu.VMEM((d, tn), BF16), pltpu.VMEM((d, tn), BF16)],
        compiler_params=_params("arbitrary", "arbitrary"),
        name="proj_glu",
    )(h, w_in, w_in)


def _seg_kernel(h_ref, w_ref, cos_ref, sin_ref, *refs, head_w, out_scale, outs):
    out_refs, w_sc = refs[:-1], refs[-1]

    @pl.when(pl.program_id(0) == 0)
    def _():
        w_sc[...] = w_ref[...].astype(BF16)

    acc = _dot(h_ref[...], w_sc[...])
    for g in range(acc.shape[1] // LANES):
        sl = slice(g * LANES, (g + 1) * LANES)
        x = acc[:, sl]
        if head_w:
            x = _rope_block(x, cos_ref[...], sin_ref[...], head_w)
        for kind, o_ref in zip(outs, out_refs):
            if kind == "f32":
                o_ref[:, sl] = x
            elif kind == "bf16":
                o_ref[:, sl] = (x * out_scale).astype(BF16)
            else:
                o_ref[sl, :] = x.T.astype(BF16)


def _proj_seg(h, w_in, col0, width, cos_t, sin_t, nb, tiles_per_seq, tm, *, head_w, out_scale=1.0,
              outs, name):
    m, d = h.shape
    cblk = col0 // width
    tab_spec = pl.BlockSpec((tm, LANES), lambda i: (i % tiles_per_seq, 0))
    out_shape, out_specs = [], []
    for kind in outs:
        if kind == "bf16_t":
            out_shape.append(jax.ShapeDtypeStruct((nb, width, m // nb), BF16))
            out_specs.append(pl.BlockSpec((None, width, tm),
                                          lambda i: (i // tiles_per_seq, 0, i % tiles_per_seq)))
        else:
            out_shape.append(jax.ShapeDtypeStruct((m, width), F32 if kind == "f32" else BF16))
            out_specs.append(pl.BlockSpec((tm, width), lambda i: (i, 0)))
    kern = functools.partial(_seg_kernel, head_w=head_w, out_scale=out_scale, outs=outs)
    return pl.pallas_call(
        kern,
        grid=(m // tm,),
        in_specs=[pl.BlockSpec((tm, d), lambda i: (i, 0)),
                  pl.BlockSpec((d, width), lambda i: (0, cblk)),
                  tab_spec, tab_spec],
        out_specs=out_specs,
        out_shape=out_shape,
        scratch_shapes=[pltpu.VMEM((d, width), BF16)],
        compiler_params=_params("arbitrary"),
        name=name,
    )(h, w_in, cos_t, sin_t)


def _kiwi_kernel(h_ref, w_ref, cos_ref, sin_ref, ki_ref, kie_ref, kio_ref, wi_ref, *maybe_wit,
                 wi_scale):
    acc = _dot(h_ref[...], w_ref[...].astype(BF16))
    roped = _rope_block(acc, cos_ref[...], sin_ref[...], D_IDX)
    ki_ref[...] = roped[:, :D_IDX]
    lane = lax.broadcasted_iota(I32, roped.shape, 1)
    even = jnp.where(lane < D_IDX, roped, 0.0)
    kie_ref[...] = even.astype(BF16)
    kio_ref[...] = pltpu.roll(even, D_IDX, 1).astype(BF16)
    wi_ref[...] = acc[:, D_IDX:D_IDX + N_IDX_HEADS] * wi_scale
    if maybe_wit:
        maybe_wit[0][...] = acc.T[D_IDX:D_IDX + N_IDX_HEADS, :] * wi_scale


def _proj_kiwi(h, w_kiwi, cos_t, sin_t, nb, tiles_per_seq, tm, wi_scale, want_t):
    m, d = h.shape
    tab_spec = pl.BlockSpec((tm, LANES), lambda i: (i % tiles_per_seq, 0))
    out_specs = [pl.BlockSpec((tm, D_IDX), lambda i: (i, 0)),
                 pl.BlockSpec((tm, LANES), lambda i: (i, 0)),
                 pl.BlockSpec((tm, LANES), lambda i: (i, 0)),
                 pl.BlockSpec((tm, N_IDX_HEADS), lambda i: (i, 0))]
    out_shape = [jax.ShapeDtypeStruct((m, D_IDX), F32),
                 jax.ShapeDtypeStruct((m, LANES), BF16),
                 jax.ShapeDtypeStruct((m, LANES), BF16),
                 jax.ShapeDtypeStruct((m, N_IDX_HEADS), F32)]
    if want_t:
        out_specs.append(pl.BlockSpec((None, N_IDX_HEADS, tm),
                                      lambda i: (i // tiles_per_seq, 0, i % tiles_per_seq)))
        out_shape.append(jax.ShapeDtypeStruct((nb, N_IDX_HEADS, m // nb), F32))
    return pl.pallas_call(
        functools.partial(_kiwi_kernel, wi_scale=wi_scale),
        grid=(m // tm,),
        in_specs=[pl.BlockSpec((tm, d), lambda i: (i, 0)),
                  pl.BlockSpec((d, LANES), lambda i: (0, 0)),
                  tab_spec, tab_spec],
        out_specs=out_specs,
        out_shape=out_shape,
        compiler_params=_params("parallel"),
        name="proj_kiwi",
    )(h, w_kiwi, cos_t, sin_t)


def _ln_swish(y, g, b):
    mu = jnp.mean(y, axis=-1, keepdims=True)
    yc = y - mu
    var = jnp.mean(yc * yc, axis=-1, keepdims=True)
    z = yc * lax.rsqrt(var + EPS) * g + b
    return z * jax.nn.sigmoid(z)


CONV_HALO = 32
CONV_ROWS = 32
CONV_COLS = 512


def _conv_kernel(cur_ref, prev_ref, w_ref, b_ref, g_ref, beta_ref, o_ref, ext_ref, y_ref):
    tm, c = cur_ref.shape
    first = pl.program_id(1) == 0
    hist = prev_ref[tm - CONV_HALO:, :]
    ext_ref[:CONV_HALO, :] = jnp.where(first, 0.0, hist)
    ext_ref[CONV_HALO:, :] = cur_ref[...]
    lead = CONV_HALO - (CONV_W - 1)
    for r0 in range(0, tm, CONV_ROWS):
        for c0 in range(0, c, CONV_COLS):
            cs = slice(c0, c0 + CONV_COLS)
            acc = jnp.broadcast_to(b_ref[:, cs], (CONV_ROWS, CONV_COLS))
            for j in range(CONV_W):
                acc = acc + w_ref[j:j + 1, cs] * ext_ref[r0 + lead + j:r0 + lead + j + CONV_ROWS, cs]
            y_ref[r0:r0 + CONV_ROWS, cs] = acc
    o_ref[...] = _ln_swish(y_ref[...], g_ref[...], beta_ref[...]).astype(o_ref.dtype)


def _conv_prompt(u, conv_w, conv_b, ln_g, ln_b, tm):
    nb, t, c = u.shape
    row = lambda a: a.reshape(1, c)
    full = lambda shape: pl.BlockSpec(shape, lambda n, i: (0, 0))
    return pl.pallas_call(
        _conv_kernel,
        grid=(nb, t // tm),
        in_specs=[pl.BlockSpec((None, tm, c), lambda n, i: (n, i, 0)),
                  pl.BlockSpec((None, tm, c), lambda n, i: (n, jnp.maximum(i - 1, 0), 0)),
                  full((CONV_W, c)), full((1, c)), full((1, c)), full((1, c))],
        out_specs=pl.BlockSpec((None, tm, c), lambda n, i: (n, i, 0)),
        out_shape=jax.ShapeDtypeStruct((nb, t, c), BF16),
        scratch_shapes=[pltpu.VMEM((tm + CONV_HALO, c), F32), pltpu.VMEM((tm, c), F32)],
        compiler_params=_params("parallel", "arbitrary"),
        name="conv_prompt",
    )(u, u, conv_w, row(conv_b), row(ln_g), row(ln_b))


def _conv_step_kernel(state_ref, u_ref, w_ref, b_ref, g_ref, beta_ref, o_ref):
    acc = b_ref[...] + w_ref[CONV_W - 1:CONV_W, :] * u_ref[...]
    for j in range(CONV_W - 1):
        acc = acc + w_ref[j:j + 1, :] * state_ref[j]
    o_ref[...] = _ln_swish(acc, g_ref[...], beta_ref[...]).astype(o_ref.dtype)


def _conv_sample(state_t, u, conv_w, conv_b, ln_g, ln_b):
    nb, c = u.shape
    row = lambda a: a.reshape(1, c)
    return pl.pallas_call(
        _conv_step_kernel,
        out_shape=jax.ShapeDtypeStruct((nb, c), BF16),
        compiler_params=pltpu.CompilerParams(vmem_limit_bytes=VMEM_LIMIT),
        name="conv_sample",
    )(state_t, u, conv_w, row(conv_b), row(ln_g), row(ln_b))


ATTN_KC = 512
ATTN_KA = 512
COUNT_LANES = 4


def _attn_prompt_kernel(qi_ref, wit_ref, kie_ref, kio_ref, q_ref, k_ref, vt_ref, o_ref,
                        sc_ref, qis_ref, qbd_ref, m_ref, l_ref, acc_ref, thr_ref, *, topk):
    b = pl.program_id(1)
    tq = q_ref.shape[0]
    kc = ATTN_KC
    n_chunks = (b * tq + tq + kc - 1) // kc
    q_pos = b * tq + lax.broadcasted_iota(I32, (1, tq), 1)
    pairs = N_HEADS // 2

    zero = jnp.zeros((tq, HEAD_DIM), BF16)
    for j in range(pairs):
        qis_ref[j, :tq, :] = qi_ref[:, (2 * j) * LANES:(2 * j + 1) * LANES]
        qis_ref[j, tq:, :] = qi_ref[:, (2 * j + 1) * LANES:(2 * j + 2) * LANES]
        qbd_ref[j, :tq, :HEAD_DIM] = q_ref[:, (2 * j) * HEAD_DIM:(2 * j + 1) * HEAD_DIM]
        qbd_ref[j, :tq, HEAD_DIM:] = zero
        qbd_ref[j, tq:, :HEAD_DIM] = zero
        qbd_ref[j, tq:, HEAD_DIM:] = q_ref[:, (2 * j + 1) * HEAD_DIM:(2 * j + 2) * HEAD_DIM]

    half = kc // 2

    def score_chunk(c, carry):
        for part in range(2):
            ks = pl.multiple_of(c * kc + part * half, half)
            ke = kie_ref[pl.ds(ks, half), :]
            ko = kio_ref[pl.ds(ks, half), :]
            acc = jnp.zeros((half, tq), F32)
            for j in range(pairs):
                de = _dot_nt(ke, qis_ref[j])
                do = _dot_nt(ko, qis_ref[j])
                acc = acc + wit_ref[4 * j:4 * j + 1, :] * jnp.maximum(de[:, :tq], 0.0)
                acc = acc + wit_ref[4 * j + 2:4 * j + 3, :] * jnp.maximum(de[:, tq:], 0.0)
                acc = acc + wit_ref[4 * j + 1:4 * j + 2, :] * jnp.maximum(do[:, :tq], 0.0)
                acc = acc + wit_ref[4 * j + 3:4 * j + 4, :] * jnp.maximum(do[:, tq:], 0.0)
            k_pos = ks + lax.broadcasted_iota(I32, (half, 1), 0)
            sc_ref[pl.ds(ks, half), :] = jnp.where(k_pos <= q_pos, acc, -jnp.inf)
        return carry

    lax.fori_loop(0, n_chunks, score_chunk, 0)

    @pl.when((b + 1) * tq <= topk)
    def _():
        thr_ref[...] = jnp.full((1, tq), -FLT_MAX, F32)

    @pl.when((b + 1) * tq > topk)
    def _():
        def count_ge(t):
            def chunk(c, parts):
                ks = pl.multiple_of(c * kc, kc)
                parts = list(parts)
                for r in range(kc // SUBLANES):
                    x = sc_ref[pl.ds(ks + r * SUBLANES, SUBLANES), :]
                    parts[r % COUNT_LANES] = parts[r % COUNT_LANES] + jnp.where(x >= t, 1.0, 0.0)
                return tuple(parts)

            init = tuple(jnp.zeros((SUBLANES, tq), F32) for _ in range(COUNT_LANES))
            parts = lax.fori_loop(0, n_chunks, chunk, init)
            return jnp.sum(sum(parts), axis=0, keepdims=True)

        thr_ref[...] = _kth_largest(count_ge, float(topk), (1, tq))

    thr = thr_ref[...]

    def bias_chunk(c, carry):
        ks = pl.multiple_of(c * kc, kc)
        sc_ref[pl.ds(ks, kc), :] = jnp.where(sc_ref[pl.ds(ks, kc), :] >= thr, 0.0, MASKED_SCORE)
        return carry

    lax.fori_loop(0, n_chunks, bias_chunk, 0)

    m_ref[...] = jnp.full(m_ref.shape, RUNNING_MAX_INIT, F32)
    l_ref[...] = jnp.zeros(l_ref.shape, F32)
    acc_ref[...] = jnp.zeros(acc_ref.shape, F32)

    ka = ATTN_KA

    def attend_chunk(c, carry):
        ks = pl.multiple_of(c * ka, ka)
        bias = sc_ref[pl.ds(ks, ka), :]
        for j in range(pairs):
            s2 = _dot_nt(k_ref[pl.ds(ks, ka), 2 * j * HEAD_DIM:(2 * j + 2) * HEAD_DIM], qbd_ref[j])
            for e in range(2):
                h = 2 * j + e
                hs = slice(h * HEAD_DIM, (h + 1) * HEAD_DIM)
                s = s2[:, e * tq:(e + 1) * tq] + bias
                m_old = m_ref[h:h + 1, :]
                m_new = jnp.maximum(m_old, jnp.max(s, axis=0, keepdims=True))
                p = jnp.exp(s - m_new)
                alpha = jnp.exp(m_old - m_new)
                l_ref[h:h + 1, :] = alpha * l_ref[h:h + 1, :] + jnp.sum(p, axis=0, keepdims=True)
                acc_ref[hs, :] = alpha * acc_ref[hs, :] + _dot(vt_ref[hs, pl.ds(ks, ka)], p.astype(BF16))
                m_ref[h:h + 1, :] = m_new
        return carry

    lax.fori_loop(0, n_chunks * (kc // ka), attend_chunk, 0)

    for h in range(N_HEADS):
        hs = slice(h * HEAD_DIM, (h + 1) * HEAD_DIM)
        o_ref[:, hs] = (acc_ref[hs, :] / l_ref[h:h + 1, :]).T.astype(o_ref.dtype)


def _attn_prompt(qi, wit, kie, kio, q, k, vt, topk):
    nb, t, aw = q.shape
    tq = Q_BLOCK
    blk = lambda w: pl.BlockSpec((None, tq, w), lambda n, b: (n, b, 0))
    seq = lambda w: pl.BlockSpec((None, t, w), lambda n, b: (n, 0, 0))
    return pl.pallas_call(
        functools.partial(_attn_prompt_kernel, topk=topk),
        grid=(nb, t // tq),
        in_specs=[blk(N_IDX_HEADS * D_IDX),
                  pl.BlockSpec((None, N_IDX_HEADS, tq), lambda n, b: (n, 0, b)),
                  seq(LANES), seq(LANES), blk(aw), seq(aw),
                  pl.BlockSpec((None, aw, t), lambda n, b: (n, 0, 0))],
        out_specs=blk(aw),
        out_shape=jax.ShapeDtypeStruct((nb, t, aw), BF16),
        scratch_shapes=[pltpu.VMEM((t, tq), F32),
                        pltpu.VMEM((N_HEADS // 2, 2 * tq, LANES), BF16),
                        pltpu.VMEM((N_HEADS // 2, 2 * tq, 2 * HEAD_DIM), BF16),
                        pltpu.VMEM((N_HEADS, tq), F32), pltpu.VMEM((N_HEADS, tq), F32),
                        pltpu.VMEM((aw, tq), F32), pltpu.VMEM((1, tq), F32)],
        compiler_params=_params("parallel", "arbitrary"),
        name="attn_prompt",
    )(qi, wit, kie, kio, q, k, vt)


IDX_PAGES_PER_STEP = 16


def _idx_pages_kernel(pt_ref, *refs):
    g_pages = IDX_PAGES_PER_STEP
    page_refs = refs[:g_pages]
    qi_ref, wi_ref, kinew_ref, o_ref, onew_ref = refs[g_pages:]
    wi = wi_ref[...]
    qi = qi_ref[...]
    for g in range(g_pages):
        d = _dot_nt(qi, page_refs[g][...].astype(BF16))
        o_ref[g:g + 1, :] = jnp.sum(wi * jnp.maximum(d, 0.0), axis=0, keepdims=True)
    d_new = jnp.sum(qi.astype(F32) * kinew_ref[...].astype(F32), axis=1, keepdims=True)
    onew_ref[...] = jnp.sum(wi * jnp.maximum(d_new, 0.0), axis=0, keepdims=True)


def _idx_scores_sample(page_table, cache_kidx, qi, wi, ki_new):
    nb, n_pages = page_table.shape
    g_pages = IDX_PAGES_PER_STEP
    page_specs = [pl.BlockSpec((None, None, PAGE_SIZE, D_IDX),
                               lambda n, p, pt, g=g: (0, pt[n, p * g_pages + g], 0, 0))
                  for g in range(g_pages)]
    return pl.pallas_call(
        _idx_pages_kernel,
        grid_spec=pltpu.PrefetchScalarGridSpec(
            num_scalar_prefetch=1,
            grid=(nb, n_pages // g_pages),
            in_specs=page_specs + [
                pl.BlockSpec((None, N_IDX_HEADS, D_IDX), lambda n, p, pt: (n, 0, 0)),
                pl.BlockSpec((None, N_IDX_HEADS, 1), lambda n, p, pt: (n, 0, 0)),
                pl.BlockSpec((None, 1, D_IDX), lambda n, p, pt: (n, 0, 0))],
            out_specs=[pl.BlockSpec((None, g_pages, PAGE_SIZE), lambda n, p, pt: (n, p, 0)),
                       pl.BlockSpec((None, 1, 1), lambda n, p, pt: (n, 0, 0))]),
        out_shape=[jax.ShapeDtypeStruct((nb, n_pages, PAGE_SIZE), F32),
                   jax.ShapeDtypeStruct((nb, 1, 1), F32)],
        compiler_params=_params("parallel", "arbitrary"),
        name="idx_scores_sample",
    )(page_table, *([cache_kidx] * g_pages), qi, wi, ki_new)


def _select_sample_kernel(sc_ref, snew_ref, sel_ref, selnew_ref, *, topk):
    scores = sc_ref[...]
    s_new = snew_ref[...]

    def count_ge(t):
        hit = jnp.where(scores >= t, 1.0, 0.0)
        return jnp.sum(hit, axis=1, keepdims=True) + jnp.where(s_new >= t, 1.0, 0.0)

    thr = _kth_largest(count_ge, float(topk), (scores.shape[0], 1))
    sel_ref[...] = jnp.where(scores >= thr, 1.0, 0.0)
    selnew_ref[...] = jnp.where(s_new >= thr, 1, 0).astype(I32)


def _select_sample(scores, s_new, topk):
    nb, past = scores.shape
    return pl.pallas_call(
        functools.partial(_select_sample_kernel, topk=topk),
        out_shape=[jax.ShapeDtypeStruct((nb, past), F32), jax.ShapeDtypeStruct((nb, 1), I32)],
        compiler_params=pltpu.CompilerParams(vmem_limit_bytes=VMEM_LIMIT),
        name="select_sample",
    )(scores, s_new)


def _compact_kernel(sel_ref, code_ref, *, n_slots):
    sel = sel_ref[...]
    n_pages = sel.shape[0]
    sel_bf = sel.astype(BF16)
    lane_r = lax.broadcasted_iota(I32, (PAGE_SIZE, PAGE_SIZE), 0)
    lane_c = lax.broadcasted_iota(I32, (PAGE_SIZE, PAGE_SIZE), 1)
    before = jnp.where(lane_r < lane_c, 1.0, 0.0).astype(BF16)
    rank_in_page = _dot(sel_bf, before)
    ones = jnp.ones((SUBLANES, PAGE_SIZE), BF16)
    cnt_row = _dot_nt(ones, sel_bf)[0:1, :]
    pg_r = lax.broadcasted_iota(I32, (n_pages, n_pages), 0)
    pg_c = lax.broadcasted_iota(I32, (n_pages, n_pages), 1)
    upto = jnp.where(pg_r <= pg_c, 1.0, 0.0).astype(BF16)
    cnt8 = jnp.broadcast_to(cnt_row, (SUBLANES, n_pages)).astype(BF16)
    incl_row = _dot(cnt8, upto)[0:1, :]
    excl_row = incl_row - cnt_row

    slot = lax.broadcasted_iota(I32, (n_slots, 1), 0).astype(F32)
    in_page = jnp.where(excl_row <= slot, jnp.where(slot < incl_row, 1.0, 0.0), 0.0)
    page_ids = lax.broadcasted_iota(I32, (1, n_pages), 1).astype(F32)
    page_of_slot = jnp.sum(in_page * page_ids, axis=1, keepdims=True)
    rank_of_slot = slot - jnp.sum(in_page * excl_row, axis=1, keepdims=True)
    ranks = jnp.where(sel > 0.5, rank_in_page, -1.0).astype(BF16)
    page_ranks = _dot(in_page.astype(BF16), ranks)
    lane_ids = lax.broadcasted_iota(I32, (1, PAGE_SIZE), 1).astype(F32)
    off_of_slot = jnp.sum(jnp.where(page_ranks == rank_of_slot, lane_ids, 0.0), axis=1, keepdims=True)
    used = jnp.sum(in_page, axis=1, keepdims=True)
    code = jnp.where(used > 0.5, page_of_slot * PAGE_SIZE + off_of_slot, -1.0)
    code_ref[...] = code.astype(I32)


def _compact_sample(sel, n_slots):
    nb, n_pages, _ = sel.shape
    return pl.pallas_call(
        functools.partial(_compact_kernel, n_slots=n_slots),
        grid=(nb,),
        in_specs=[pl.BlockSpec((None, n_pages, PAGE_SIZE), lambda n: (n, 0, 0))],
        out_specs=pl.BlockSpec((None, n_slots, 1), lambda n: (n, 0, 0)),
        out_shape=jax.ShapeDtypeStruct((nb, n_slots, 1), I32),
        compiler_params=_params("parallel"),
        name="compact_sample",
    )(sel)


def _attn_sample_kernel(pt_ref, code_ref, selnew_ref, ck_ref, cv_ref, q_ref, kn_ref, vn_ref, o_ref,
                        kbuf, vbuf, ksem, vsem):
    n = pl.program_id(0)
    n_seq = pl.num_programs(0)
    n_slots = kbuf.shape[1]

    def start_rows(seq, slot):
        def body(s, carry):
            code = jnp.maximum(code_ref[seq, s], 0)
            phys = pt_ref[seq, lax.shift_right_logical(code, jnp.int32(PAGE_SHIFT))]
            off = code & (PAGE_SIZE - 1)
            pltpu.make_async_copy(ck_ref.at[0, phys, off], kbuf.at[slot, s], ksem.at[slot]).start()
            pltpu.make_async_copy(cv_ref.at[0, phys, off], vbuf.at[slot, s], vsem.at[slot]).start()
            return carry
        lax.fori_loop(0, n_slots, body, 0, unroll=8)

    def wait_rows(seq, slot):
        pltpu.make_async_copy(kbuf.at[slot], kbuf.at[slot], ksem.at[slot]).wait()
        pltpu.make_async_copy(vbuf.at[slot], vbuf.at[slot], vsem.at[slot]).wait()

    cur = n % 2

    @pl.when(n == 0)
    def _():
        start_rows(0, 0)

    @pl.when(n + 1 < n_seq)
    def _():
        start_rows(n + 1, 1 - cur)

    wait_rows(n, cur)

    n_used = jnp.where(code_ref[n, n_slots - 1] < 0, n_slots - 1, n_slots)
    new_on = selnew_ref[n] > 0
    q = q_ref[...]
    rnd = lambda a: a.astype(BF16).astype(F32)
    s = jnp.sum(rnd(kbuf[cur]) * q[None], axis=-1, keepdims=True)
    live = lax.broadcasted_iota(I32, s.shape, 0) < n_used
    s = jnp.where(live, s, MASKED_SCORE)
    s_new = jnp.sum(rnd(kn_ref[...]) * q, axis=-1, keepdims=True)
    s_new = jnp.where(new_on, s_new, MASKED_SCORE)
    m = jnp.maximum(jnp.max(s, axis=0), s_new)
    p = jnp.where(live, jnp.exp(s - m[None]), 0.0)
    p_new = jnp.where(new_on, jnp.exp(s_new - m), 0.0)
    l = jnp.sum(p, axis=0) + p_new
    acc = jnp.sum(rnd(p) * rnd(vbuf[cur]), axis=0) + rnd(p_new) * rnd(vn_ref[...])
    o_ref[...] = acc / l


def _attn_sample(page_table, codes, sel_new, cache_k, cache_v, q, k_new, v_new):
    nb, n_slots = codes.shape
    per_seq = pl.BlockSpec((None, N_HEADS, HEAD_DIM), lambda n, *_: (n, 0, 0))
    hbm = pl.BlockSpec(memory_space=pl.ANY)
    return pl.pallas_call(
        _attn_sample_kernel,
        grid_spec=pltpu.PrefetchScalarGridSpec(
            num_scalar_prefetch=3,
            grid=(nb,),
            in_specs=[hbm, hbm, per_seq, per_seq, per_seq],
            out_specs=per_seq,
            scratch_shapes=[pltpu.VMEM((2, n_slots, N_HEADS, HEAD_DIM), F32),
                            pltpu.VMEM((2, n_slots, N_HEADS, HEAD_DIM), F32),
                            pltpu.SemaphoreType.DMA((2,)), pltpu.SemaphoreType.DMA((2,))]),
        out_shape=jax.ShapeDtypeStruct((nb, N_HEADS, HEAD_DIM), F32),
        compiler_params=_params("arbitrary"),
        name="attn_sample",
    )(page_table, codes, sel_new, cache_k, cache_v, q, k_new, v_new)


def _outproj_kernel(a_ref, b_ref, w_ref, x_ref, gate_ref, o_ref, w_sc):
    @pl.when(pl.program_id(1) == 0)
    def _():
        w_sc[...] = w_ref[...].astype(BF16)

    c = a_ref.shape[1]
    mix = _dot(a_ref[...], w_sc[:c, :]) + _dot(b_ref[...], w_sc[c:, :])
    o_ref[...] = x_ref[...] + gate_ref[...] * mix


def _outproj(a, b, w_out, x, gate, per_row, tm, tiles_per_batch):
    m, c = a.shape
    d = x.shape[1]
    tn = 1024
    if per_row:
        gate_spec = pl.BlockSpec((tm, tn), lambda j, i: (i, j))
    else:
        gate_spec = pl.BlockSpec((None, 1, tn), lambda j, i: (i // tiles_per_batch, 0, j))
    return pl.pallas_call(
        _outproj_kernel,
        grid=(d // tn, m // tm),
        in_specs=[pl.BlockSpec((tm, c), lambda j, i: (i, 0)),
                  pl.BlockSpec((tm, b.shape[1]), lambda j, i: (i, 0)),
                  pl.BlockSpec((w_out.shape[0], tn), lambda j, i: (0, j)),
                  pl.BlockSpec((tm, tn), lambda j, i: (i, j)),
                  gate_spec],
        out_specs=pl.BlockSpec((tm, tn), lambda j, i: (i, j)),
        out_shape=jax.ShapeDtypeStruct((m, d), F32),
        scratch_shapes=[pltpu.VMEM((w_out.shape[0], tn), BF16)],
        compiler_params=_params("arbitrary", "arbitrary"),
        name="outproj",
    )(a, b, w_out, x, gate)


def _mlp_kernel(h_ref, w1_ref, w2_ref, x_ref, gate_ref, fg_ref, o_ref):
    j = pl.program_id(1)
    hid = _dot(h_ref[...], w1_ref[...])
    hid = jnp.square(jnp.maximum(hid, 0.0)).astype(BF16)
    part = _dot(hid, w2_ref[...])

    @pl.when(j == 0)
    def _():
        o_ref[...] = part

    @pl.when(j > 0)
    def _():
        o_ref[...] += part

    @pl.when(j == pl.num_programs(1) - 1)
    def _():
        x = x_ref[...] + gate_ref[...] * o_ref[...]
        y = x * lax.rsqrt(jnp.mean(x * x, axis=-1, keepdims=True) + EPS)
        o_ref[...] = y * fg_ref[...]


def _mlp(h, w1, w2, x, gate, final_g, per_row, tm, tiles_per_batch):
    m, d = h.shape
    f = w1.shape[1]
    tf = 512
    ms = _mod_spec(per_row, tm, tiles_per_batch, d)
    return pl.pallas_call(
        _mlp_kernel,
        grid=(m // tm, f // tf),
        in_specs=[pl.BlockSpec((tm, d), lambda i, j: (i, 0)),
                  pl.BlockSpec((d, tf), lambda i, j: (0, j)),
                  pl.BlockSpec((tf, d), lambda i, j: (j, 0)),
                  pl.BlockSpec((tm, d), lambda i, j: (i, 0)),
                  ms,
                  pl.BlockSpec((1, d), lambda i, j: (0, 0))],
        out_specs=pl.BlockSpec((tm, d), lambda i, j: (i, 0)),
        out_shape=jax.ShapeDtypeStruct((m, d), F32),
        compiler_params=_params("parallel", "arbitrary"),
        name="mlp",
    )(h, w1, w2, x, gate, final_g.reshape(1, d))


def _split_mod(mod, per_row):
    parts = jnp.split(mod, 6, axis=-1)
    return parts if per_row else [p[:, None, :] for p in parts]


def _trunk(x, nb, mod, per_row, pos, tm, weights, conv_fn, attend_fn):
    (norm1_g, w_in, w_out, norm2_g, w1_bf, w2_bf, final_g) = weights
    m, d = x.shape
    c_conv = d // 2
    attn_w = N_HEADS * HEAD_DIM
    shift1, scale1, gate1, shift2, scale2, gate2 = _split_mod(mod, per_row)
    tiles_per_seq = pos.shape[0] // tm
    tiles_per_batch = (m // nb) // tm if not per_row else 1

    h = _prenorm(x, norm1_g, shift1, scale1, per_row, tm, tiles_per_batch)
    u = _proj_glu(h, w_in, c_conv, tm)
    cos_h, sin_h = _rope_tables(pos, HEAD_DIM)
    cos_i, sin_i = _rope_tables(pos, D_IDX)
    seg = functools.partial(_proj_seg, h, w_in, nb=nb, tiles_per_seq=tiles_per_seq, tm=tm)
    col_q = 2 * c_conv
    (q_bf,) = seg(col_q, attn_w, cos_h, sin_h, head_w=HEAD_DIM, out_scale=HEAD_DIM ** -0.5,
                  outs=("bf16",), name="proj_q")
    k_f32, k_bf = seg(col_q + attn_w, attn_w, cos_h, sin_h, head_w=HEAD_DIM,
                      outs=("f32", "bf16"), name="proj_k")
    v_outs = seg(col_q + 2 * attn_w, attn_w, cos_h, sin_h, head_w=0,
                 outs=("f32",) if per_row else ("f32", "bf16_t"), name="proj_v")
    (qi_bf,) = seg(col_q + 3 * attn_w, N_IDX_HEADS * D_IDX, cos_i, sin_i, head_w=D_IDX,
                   outs=("bf16",), name="proj_qi")
    col_ki = col_q + 3 * attn_w + N_IDX_HEADS * D_IDX
    w_kiwi = jnp.pad(w_in[:, col_ki:], ((0, 0), (0, LANES - (w_in.shape[1] - col_ki))))
    kiwi = _proj_kiwi(h, w_kiwi, cos_i, sin_i, nb, tiles_per_seq, tm,
                      (D_IDX ** -0.5) * (N_IDX_HEADS ** -0.5), want_t=not per_row)

    a_out = conv_fn(u)
    b_out = attend_fn(q_bf, k_f32, k_bf, v_outs, qi_bf, kiwi)
    x1 = _outproj(a_out, b_out, w_out, x, gate1, per_row, tm, tiles_per_batch)
    h2 = _prenorm(x1, norm2_g, shift2, scale2, per_row, tm, tiles_per_batch)
    y = _mlp(h2, w1_bf, w2_bf, x1, gate2, final_g, per_row, tm, tiles_per_batch)
    return y, k_f32, v_outs[0], kiwi[0], u


def kernel(x_prompt, x_sample, cache_k, cache_v, cache_kidx, state_conv, page_table, c_prompt, c_sample,
           norm1_g, w_ada, b_ada, w_in, conv_w, conv_b, conv_ln_g, conv_ln_b, w_out, norm2_g,
           w_ff1, w_ff2, final_g):
    nb, t, d = x_prompt.shape
    ns, ts, _ = x_sample.shape
    assert ts == 1 and norm1_g.shape[0] == 1, "one new token per sequence, depth one"
    c_conv = d // 2
    attn_w = N_HEADS * HEAD_DIM
    n_pages = page_table.shape[1]
    past = n_pages * PAGE_SIZE

    rows = nb + ns
    pad = (-rows) % SUBLANES
    c_all = jnp.pad(jnp.concatenate([c_prompt, c_sample], axis=0), ((0, pad), (0, 0)))
    mod = _ada(c_all, w_ada[0], b_ada[0])
    weights = (norm1_g[0], w_in[0], w_out[0], norm2_g[0],
               w_ff1[0].astype(BF16), w_ff2[0].astype(BF16), final_g)
    conv_params = (conv_w[0], conv_b[0], conv_ln_g[0], conv_ln_b[0])

    tm_p = 512
    topk_p = min(TOPK_MAX, t // 4)

    def conv_p(u):
        return _conv_prompt(u.reshape(nb, t, c_conv), *conv_params, 256).reshape(nb * t, c_conv)

    def attend_p(q_bf, k_f32, k_bf, v_outs, qi_bf, kiwi):
        _, kie, kio, _, wit = kiwi
        r3 = lambda a: a.reshape(nb, t, a.shape[-1])
        o = _attn_prompt(r3(qi_bf), wit, r3(kie), r3(kio), r3(q_bf), r3(k_bf), v_outs[1], topk_p)
        return o.reshape(nb * t, attn_w)

    yp, kp, vp, kip, up = _trunk(x_prompt.reshape(nb * t, d), nb, mod[:nb], False, jnp.arange(t), tm_p,
                                 weights, conv_p, attend_p)

    topk_s = min(TOPK_MAX, (past + 1) // 4)
    state_t = jnp.swapaxes(state_conv[0], 0, 1)

    def conv_s(u):
        return _conv_sample(state_t, u, *conv_params)

    def attend_s(q_bf, k_f32, k_bf, v_outs, qi_bf, kiwi):
        _, kie, _, wi = kiwi
        qi3 = qi_bf.reshape(ns, N_IDX_HEADS, D_IDX)
        scores, s_new = _idx_scores_sample(page_table, cache_kidx, qi3, wi[:, :, None],
                                           kie[:, None, :D_IDX])
        sel, sel_new = _select_sample(scores.reshape(ns, past), s_new.reshape(ns, 1), topk_s)
        codes = _compact_sample(sel.reshape(ns, n_pages, PAGE_SIZE), topk_s)
        per_head = lambda a: a.astype(F32).reshape(ns, N_HEADS, HEAD_DIM)
        o = _attn_sample(page_table, codes.reshape(ns, topk_s), sel_new.reshape(ns), cache_k, cache_v,
                         per_head(q_bf), per_head(k_f32), per_head(v_outs[0]))
        return o.reshape(ns, attn_w).astype(BF16)

    ys, ks, vs, kis, us = _trunk(x_sample.reshape(ns, d), ns, mod[nb:nb + ns], True,
                                 jnp.full((ns,), past, jnp.int32), ns, weights, conv_s, attend_s)

    heads = lambda a, n, tt: a.reshape(1, n, tt, N_HEADS, HEAD_DIM)
    conv_prompt = up.reshape(nb, t, c_conv)[:, t - (CONV_W - 1):][None]
    conv_sample = jnp.concatenate([state_conv[0][:, 1:], us[:, None, :]], axis=1)[None]
    return (yp.reshape(nb, t, d), ys.reshape(ns, 1, d),
            heads(kp, nb, t), heads(vp, nb, t), kip.reshape(1, nb, t, D_IDX), conv_prompt,
            heads(ks, ns, 1), heads(vs, ns, 1), kis.reshape(1, ns, 1, D_IDX), conv_sample)
```

```python
import functools

import jax
import jax.numpy as jnp
from jax import lax
from jax.experimental import pallas as pl
from jax.experimental.pallas import tpu as pltpu

F32 = jnp.float32
BF16 = jnp.bfloat16
I32 = jnp.int32

N_HEADS = 8
HEAD_DIM = 128
N_IDX_HEADS = 16
D_IDX = 64
CONV_W = 31
TOPK_MAX = 256
PAGE_SIZE = 128
PAGE_SHIFT = PAGE_SIZE.bit_length() - 1
Q_BLOCK = 128
ROPE_THETA = 10000.0
EPS = 1e-6

LANES = 128
SUBLANES = 8
VMEM_LIMIT = 56 * 1024 * 1024
MASKED_SCORE = -1e30
RUNNING_MAX_INIT = -1e20
FLT_MAX = 3.4028234663852886e38
INT_MIN = -2147483648


def _params(*sem):
    return pltpu.CompilerParams(dimension_semantics=sem, vmem_limit_bytes=VMEM_LIMIT)


def _dot(a, b):
    return jnp.dot(a, b, preferred_element_type=F32)


def _dot_nt(a, b):
    return lax.dot_general(a, b, (((1,), (1,)), ((), ())), preferred_element_type=F32)


def _ordered_to_float(u):
    key = u ^ jnp.int32(INT_MIN)
    bits = key ^ ((key >> 31) & jnp.int32(0x7FFFFFFF))
    return pltpu.bitcast(bits, F32)


def _kth_largest(count_ge, topk, shape):
    def step(i, res):
        cand = res | lax.shift_left(jnp.int32(1), 31 - i)
        return jnp.where(count_ge(_ordered_to_float(cand)) >= topk, cand, res)

    res = lax.fori_loop(0, 32, step, jnp.zeros(shape, I32))
    finite = lax.shift_right_logical(res, jnp.full(shape, 23, I32)) != 0
    return jnp.where(finite, _ordered_to_float(res), -FLT_MAX)


def _ada_kernel(c_ref, w_ref, b_ref, o_ref):
    c = c_ref[...]
    a = (c * jax.nn.sigmoid(c)).astype(BF16)
    o_ref[...] = _dot(a, w_ref[...].astype(BF16)) + b_ref[...]


def _ada(c_all, w_ada, b_ada):
    r, d = c_all.shape
    n = w_ada.shape[1]
    tn = 1024
    return pl.pallas_call(
        _ada_kernel,
        grid=(n // tn,),
        in_specs=[pl.BlockSpec((r, d), lambda j: (0, 0)),
                  pl.BlockSpec((d, tn), lambda j: (0, j)),
                  pl.BlockSpec((1, tn), lambda j: (0, j))],
        out_specs=pl.BlockSpec((r, tn), lambda j: (0, j)),
        out_shape=jax.ShapeDtypeStruct((r, n), F32),
        compiler_params=_params("arbitrary"),
        name="ada",
    )(c_all, w_ada, b_ada.reshape(1, n))


def _mod_spec(per_row, tm, tiles_per_batch, d):
    if per_row:
        return pl.BlockSpec((tm, d), lambda i, *_: (i, 0))
    return pl.BlockSpec((None, 1, d), lambda i, *_: (i // tiles_per_batch, 0, 0))


def _prenorm_kernel(x_ref, g_ref, shift_ref, scale_ref, o_ref):
    x = x_ref[...]
    y = x * lax.rsqrt(jnp.mean(x * x, axis=-1, keepdims=True) + EPS) * g_ref[...]
    o_ref[...] = (y * (1.0 + scale_ref[...]) + shift_ref[...]).astype(o_ref.dtype)


def _prenorm(x, g, shift, scale, per_row, tm, tiles_per_batch):
    m, d = x.shape
    ms = _mod_spec(per_row, tm, tiles_per_batch, d)
    return pl.pallas_call(
        _prenorm_kernel,
        grid=(m // tm,),
        in_specs=[pl.BlockSpec((tm, d), lambda i: (i, 0)),
                  pl.BlockSpec((1, d), lambda i: (0, 0)), ms, ms],
        out_specs=pl.BlockSpec((tm, d), lambda i: (i, 0)),
        out_shape=jax.ShapeDtypeStruct((m, d), BF16),
        compiler_params=_params("parallel"),
        name="prenorm",
    )(x, g.reshape(1, d), shift, scale)


def _rope_tables(pos, head_w):
    half = head_w // 2
    inv_freq = ROPE_THETA ** (-jnp.arange(half, dtype=F32) / half)
    ang = pos.astype(F32)[:, None] * inv_freq[None, :]
    cos, sin = jnp.cos(ang), jnp.sin(ang)
    reps = LANES // head_w
    cos_t = jnp.tile(jnp.concatenate([cos, cos], axis=-1), (1, reps))
    sin_t = jnp.tile(jnp.concatenate([-sin, sin], axis=-1), (1, reps))
    return cos_t, sin_t


def _rope_block(x, cos, sin, head_w):
    if head_w == LANES:
        r = pltpu.roll(x, LANES // 2, 1)
    else:
        lane = lax.broadcasted_iota(I32, x.shape, 1)
        lower = (lane % head_w) < (head_w // 2)
        r = jnp.where(lower, pltpu.roll(x, LANES - head_w // 2, 1), pltpu.roll(x, head_w // 2, 1))
    return x * cos + r * sin


def _glu_kernel(h_ref, wa_ref, wg_ref, u_ref, wa_sc, wg_sc):
    @pl.when(pl.program_id(1) == 0)
    def _():
        wa_sc[...] = wa_ref[...].T.astype(BF16)
        wg_sc[...] = wg_ref[...].T.astype(BF16)

    h = h_ref[...]
    a = _dot(h, wa_sc[...])
    g = _dot(h, wg_sc[...])
    u_ref[...] = a * jax.nn.sigmoid(g)


def _proj_glu(h, w_in_t, c_conv, tm):
    m, d = h.shape
    tn = 512
    nc = c_conv // tn
    return pl.pallas_call(
        _glu_kernel,
        grid=(nc, m // tm),
        in_specs=[pl.BlockSpec((tm, d), lambda c, i: (i, 0)),
                  pl.BlockSpec((tn, d), lambda c, i: (c, 0)),
                  pl.BlockSpec((tn, d), lambda c, i: (nc + c, 0))],
        out_specs=pl.BlockSpec((tm, tn), lambda c, i: (i, c)),
        out_shape=jax.ShapeDtypeStruct((m, c_conv), F32),
        scratch_shapes=[pltpu.VMEM((d, tn), BF16), pltpu.VMEM((d, tn), BF16)],
        compiler_params=_params("arbitrary", "arbitrary"),
        name="proj_glu",
    )(h, w_in_t, w_in_t)


def _seg_kernel(h_ref, w_ref, cos_ref, sin_ref, *refs, head_w, out_scale, outs):
    out_refs, w_sc = refs[:-1], refs[-1]

    @pl.when(pl.program_id(0) == 0)
    def _():
        w_sc[...] = w_ref[...].T.astype(BF16)

    acc = _dot(h_ref[...], w_sc[...])
    for g in range(acc.shape[1] // LANES):
        sl = slice(g * LANES, (g + 1) * LANES)
        x = acc[:, sl]
        if head_w:
            x = _rope_block(x, cos_ref[...], sin_ref[...], head_w)
        for kind, o_ref in zip(outs, out_refs):
            if kind == "f32":
                o_ref[:, sl] = x
            elif kind == "bf16":
                o_ref[:, sl] = (x * out_scale).astype(BF16)
            else:
                o_ref[sl, :] = x.T.astype(BF16)


def _proj_seg(h, w_in_t, col0, width, cos_t, sin_t, nb, tiles_per_seq, tm, *, head_w, out_scale=1.0,
              outs, name):
    m, d = h.shape
    cblk = col0 // width
    tab_spec = pl.BlockSpec((tm, LANES), lambda i: (i % tiles_per_seq, 0))
    out_shape, out_specs = [], []
    for kind in outs:
        if kind == "bf16_t":
            out_shape.append(jax.ShapeDtypeStruct((nb, width, m // nb), BF16))
            out_specs.append(pl.BlockSpec((None, width, tm),
                                          lambda i: (i // tiles_per_seq, 0, i % tiles_per_seq)))
        else:
            out_shape.append(jax.ShapeDtypeStruct((m, width), F32 if kind == "f32" else BF16))
            out_specs.append(pl.BlockSpec((tm, width), lambda i: (i, 0)))
    kern = functools.partial(_seg_kernel, head_w=head_w, out_scale=out_scale, outs=outs)
    return pl.pallas_call(
        kern,
        grid=(m // tm,),
        in_specs=[pl.BlockSpec((tm, d), lambda i: (i, 0)),
                  pl.BlockSpec((width, d), lambda i: (cblk, 0)),
                  tab_spec, tab_spec],
        out_specs=out_specs,
        out_shape=out_shape,
        scratch_shapes=[pltpu.VMEM((d, width), BF16)],
        compiler_params=_params("arbitrary"),
        name=name,
    )(h, w_in_t, cos_t, sin_t)


def _kiwi_kernel(h_ref, w_ref, cos_ref, sin_ref, ki_ref, kie_ref, kio_ref, wi_ref, *maybe_wit,
                 wi_scale):
    acc = _dot(h_ref[...], w_ref[...].T.astype(BF16))
    roped = _rope_block(acc, cos_ref[...], sin_ref[...], D_IDX)
    ki_ref[...] = roped[:, :D_IDX]
    lane = lax.broadcasted_iota(I32, roped.shape, 1)
    even = jnp.where(lane < D_IDX, roped, 0.0)
    kie_ref[...] = even.astype(BF16)
    kio_ref[...] = pltpu.roll(even, D_IDX, 1).astype(BF16)
    wi_ref[...] = acc[:, D_IDX:D_IDX + N_IDX_HEADS] * wi_scale
    if maybe_wit:
        maybe_wit[0][...] = acc.T[D_IDX:D_IDX + N_IDX_HEADS, :] * wi_scale


def _proj_kiwi(h, w_kiwi, cos_t, sin_t, nb, tiles_per_seq, tm, wi_scale, want_t):
    m, d = h.shape
    tab_spec = pl.BlockSpec((tm, LANES), lambda i: (i % tiles_per_seq, 0))
    out_specs = [pl.BlockSpec((tm, D_IDX), lambda i: (i, 0)),
                 pl.BlockSpec((tm, LANES), lambda i: (i, 0)),
                 pl.BlockSpec((tm, LANES), lambda i: (i, 0)),
                 pl.BlockSpec((tm, N_IDX_HEADS), lambda i: (i, 0))]
    out_shape = [jax.ShapeDtypeStruct((m, D_IDX), F32),
                 jax.ShapeDtypeStruct((m, LANES), BF16),
                 jax.ShapeDtypeStruct((m, LANES), BF16),
                 jax.ShapeDtypeStruct((m, N_IDX_HEADS), F32)]
    if want_t:
        out_specs.append(pl.BlockSpec((None, N_IDX_HEADS, tm),
                                      lambda i: (i // tiles_per_seq, 0, i % tiles_per_seq)))
        out_shape.append(jax.ShapeDtypeStruct((nb, N_IDX_HEADS, m // nb), F32))
    return pl.pallas_call(
        functools.partial(_kiwi_kernel, wi_scale=wi_scale),
        grid=(m // tm,),
        in_specs=[pl.BlockSpec((tm, d), lambda i: (i, 0)),
                  pl.BlockSpec((LANES, d), lambda i: (0, 0)),
                  tab_spec, tab_spec],
        out_specs=out_specs,
        out_shape=out_shape,
        compiler_params=_params("parallel"),
        name="proj_kiwi",
    )(h, w_kiwi, cos_t, sin_t)


def _ln_swish(y, g, b):
    mu = jnp.mean(y, axis=-1, keepdims=True)
    yc = y - mu
    var = jnp.mean(yc * yc, axis=-1, keepdims=True)
    z = yc * lax.rsqrt(var + EPS) * g + b
    return z * jax.nn.sigmoid(z)


CONV_HALO = 32
CONV_ROWS = 128
CONV_TAIL = 8
CONV_SLACK = 2 * SUBLANES


def _conv_kernel(cur_ref, prev_ref, w_ref, b_ref, g_ref, beta_ref, o_ref, ext_ref, y_ref):
    tm, c = cur_ref.shape
    first = pl.program_id(1) == 0
    hist = prev_ref[tm - CONV_HALO:, :]
    ext_ref[:CONV_HALO, :] = jnp.where(first, 0.0, hist)
    ext_ref[CONV_HALO:CONV_HALO + tm, :] = cur_ref[...]
    ext_ref[CONV_HALO + tm:, :] = jnp.zeros((CONV_TAIL, c), F32)
    lead = CONV_HALO - (CONV_W - 1)
    win = CONV_ROWS + CONV_SLACK

    def lane_block(cb, carry):
        cs = pl.ds(pl.multiple_of(cb * LANES, LANES), LANES)
        for r0 in range(0, tm, CONV_ROWS):
            y = jnp.broadcast_to(b_ref[:, cs], (CONV_ROWS, LANES))
            for b in range(SUBLANES):
                z = None
                for j in range(b, CONV_W, SUBLANES):
                    a = j - b
                    term = w_ref[j:j + 1, cs] * ext_ref[r0 + a:r0 + a + win, cs]
                    z = term if z is None else z + term
                shift = lead + b
                if shift % SUBLANES == 0:
                    y = y + z[shift:shift + CONV_ROWS]
                else:
                    y = y + pltpu.roll(z, win - shift, 0)[:CONV_ROWS]
            y_ref[r0:r0 + CONV_ROWS, cs] = y
        return carry

    lax.fori_loop(0, c // LANES, lane_block, 0)
    o_ref[...] = _ln_swish(y_ref[...], g_ref[...], beta_ref[...]).astype(o_ref.dtype)


def _conv_prompt(u, conv_w, conv_b, ln_g, ln_b, tm):
    nb, t, c = u.shape
    row = lambda a: a.reshape(1, c)
    full = lambda shape: pl.BlockSpec(shape, lambda n, i: (0, 0))
    return pl.pallas_call(
        _conv_kernel,
        grid=(nb, t // tm),
        in_specs=[pl.BlockSpec((None, tm, c), lambda n, i: (n, i, 0)),
                  pl.BlockSpec((None, tm, c), lambda n, i: (n, jnp.maximum(i - 1, 0), 0)),
                  full((CONV_W, c)), full((1, c)), full((1, c)), full((1, c))],
        out_specs=pl.BlockSpec((None, tm, c), lambda n, i: (n, i, 0)),
        out_shape=jax.ShapeDtypeStruct((nb, t, c), BF16),
        scratch_shapes=[pltpu.VMEM((CONV_HALO + tm + CONV_TAIL, c), F32), pltpu.VMEM((tm, c), F32)],
        compiler_params=_params("parallel", "arbitrary"),
        name="conv_prompt",
    )(u, u, conv_w, row(conv_b), row(ln_g), row(ln_b))


def _conv_step_kernel(state_ref, u_ref, w_ref, b_ref, g_ref, beta_ref, o_ref):
    acc = b_ref[...] + w_ref[CONV_W - 1:CONV_W, :] * u_ref[...]
    for j in range(CONV_W - 1):
        acc = acc + w_ref[j:j + 1, :] * state_ref[j]
    o_ref[...] = _ln_swish(acc, g_ref[...], beta_ref[...]).astype(o_ref.dtype)


def _conv_sample(state_t, u, conv_w, conv_b, ln_g, ln_b):
    nb, c = u.shape
    row = lambda a: a.reshape(1, c)
    return pl.pallas_call(
        _conv_step_kernel,
        out_shape=jax.ShapeDtypeStruct((nb, c), BF16),
        compiler_params=pltpu.CompilerParams(vmem_limit_bytes=VMEM_LIMIT),
        name="conv_sample",
    )(state_t, u, conv_w, row(conv_b), row(ln_g), row(ln_b))


ATTN_KC = 512
COUNT_LANES = 4


def _attn_prompt_kernel(qi_ref, wit_ref, kie_ref, kio_ref, q_ref, k_ref, vt_ref, o_ref,
                        sc_ref, qis_ref, qbd_ref, m_ref, l_ref, acc_ref, thr_ref, s0_ref, s1_ref, *, topk):
    b = pl.program_id(1)
    tq = q_ref.shape[0]
    kc = ATTN_KC
    n_chunks = (b * tq + tq + kc - 1) // kc
    q_pos = b * tq + lax.broadcasted_iota(I32, (1, tq), 1)
    pairs = N_HEADS // 2

    zero = jnp.zeros((tq, HEAD_DIM), BF16)
    for j in range(pairs):
        qis_ref[j, :tq, :] = qi_ref[:, (2 * j) * LANES:(2 * j + 1) * LANES]
        qis_ref[j, tq:, :] = qi_ref[:, (2 * j + 1) * LANES:(2 * j + 2) * LANES]
        qbd_ref[j, :tq, :HEAD_DIM] = q_ref[:, (2 * j) * HEAD_DIM:(2 * j + 1) * HEAD_DIM]
        qbd_ref[j, :tq, HEAD_DIM:] = zero
        qbd_ref[j, tq:, :HEAD_DIM] = zero
        qbd_ref[j, tq:, HEAD_DIM:] = q_ref[:, (2 * j + 1) * HEAD_DIM:(2 * j + 2) * HEAD_DIM]

    half = kc // 2

    def score_chunk(c, carry):
        for part in range(2):
            ks = pl.multiple_of(c * kc + part * half, half)
            ke = kie_ref[pl.ds(ks, half), :]
            ko = kio_ref[pl.ds(ks, half), :]
            acc = jnp.zeros((half, tq), F32)
            for j in range(pairs):
                de = _dot_nt(ke, qis_ref[j])
                do = _dot_nt(ko, qis_ref[j])
                acc = acc + wit_ref[4 * j:4 * j + 1, :] * jnp.maximum(de[:, :tq], 0.0)
                acc = acc + wit_ref[4 * j + 2:4 * j + 3, :] * jnp.maximum(de[:, tq:], 0.0)
                acc = acc + wit_ref[4 * j + 1:4 * j + 2, :] * jnp.maximum(do[:, :tq], 0.0)
                acc = acc + wit_ref[4 * j + 3:4 * j + 4, :] * jnp.maximum(do[:, tq:], 0.0)
            k_pos = ks + lax.broadcasted_iota(I32, (half, 1), 0)
            sc_ref[pl.ds(ks, half), :] = jnp.where(k_pos <= q_pos, acc, -jnp.inf)
        return carry

    lax.fori_loop(0, n_chunks, score_chunk, 0)

    @pl.when((b + 1) * tq <= topk)
    def _():
        thr_ref[...] = jnp.full((1, tq), -FLT_MAX, F32)

    @pl.when((b + 1) * tq > topk)
    def _():
        def count_ge(t):
            def chunk(c, parts):
                ks = pl.multiple_of(c * kc, kc)
                parts = list(parts)
                for r in range(kc // SUBLANES):
                    x = sc_ref[pl.ds(ks + r * SUBLANES, SUBLANES), :]
                    parts[r % COUNT_LANES] = parts[r % COUNT_LANES] + jnp.where(x >= t, 1.0, 0.0)
                return tuple(parts)

            init = tuple(jnp.zeros((SUBLANES, tq), F32) for _ in range(COUNT_LANES))
            parts = lax.fori_loop(0, n_chunks, chunk, init)
            return jnp.sum(sum(parts), axis=0, keepdims=True)

        thr_ref[...] = _kth_largest(count_ge, float(topk), (1, tq))

    thr = thr_ref[...]

    def bias_chunk(c, carry):
        ks = pl.multiple_of(c * kc, kc)
        sc_ref[pl.ds(ks, kc), :] = jnp.where(sc_ref[pl.ds(ks, kc), :] >= thr, 0.0, MASKED_SCORE)
        return carry

    lax.fori_loop(0, n_chunks, bias_chunk, 0)

    m_ref[...] = jnp.full(m_ref.shape, RUNNING_MAX_INIT, F32)
    l_ref[...] = jnp.zeros(l_ref.shape, F32)
    acc_ref[...] = jnp.zeros(acc_ref.shape, F32)

    last = n_chunks - 1

    def qk_scores(c, s_ref):
        ks = pl.multiple_of(jnp.minimum(c, last) * kc, kc)
        for j in range(pairs):
            s_ref[j] = _dot_nt(k_ref[pl.ds(ks, kc), 2 * j * HEAD_DIM:(2 * j + 2) * HEAD_DIM], qbd_ref[j])

    def softmax_pv(c, s_ref):
        ks = pl.multiple_of(c * kc, kc)
        bias = sc_ref[pl.ds(ks, kc), :]
        for h in range(N_HEADS):
            hs = slice(h * HEAD_DIM, (h + 1) * HEAD_DIM)
            s = s_ref[h // 2, :, (h % 2) * tq:(h % 2 + 1) * tq] + bias
            m_old = m_ref[h:h + 1, :]
            m_new = jnp.maximum(m_old, jnp.max(s, axis=0, keepdims=True))
            p = jnp.exp(s - m_new)
            alpha = jnp.exp(m_old - m_new)
            l_ref[h:h + 1, :] = alpha * l_ref[h:h + 1, :] + jnp.sum(p, axis=0, keepdims=True)
            acc_ref[hs, :] = alpha * acc_ref[hs, :] + _dot(vt_ref[hs, pl.ds(ks, kc)], p.astype(BF16))
            m_ref[h:h + 1, :] = m_new

    qk_scores(0, s0_ref)

    def attend_two_chunks(i, carry):
        c = 2 * i
        qk_scores(c + 1, s1_ref)
        softmax_pv(c, s0_ref)

        @pl.when(c + 1 <= last)
        def _():
            qk_scores(c + 2, s0_ref)
            softmax_pv(c + 1, s1_ref)

        return carry

    lax.fori_loop(0, (n_chunks + 1) // 2, attend_two_chunks, 0)

    for h in range(N_HEADS):
        hs = slice(h * HEAD_DIM, (h + 1) * HEAD_DIM)
        o_ref[:, hs] = (acc_ref[hs, :] / l_ref[h:h + 1, :]).T.astype(o_ref.dtype)


def _attn_prompt(qi, wit, kie, kio, q, k, vt, topk):
    nb, t, aw = q.shape
    tq = Q_BLOCK
    blk = lambda w: pl.BlockSpec((None, tq, w), lambda n, b: (n, b, 0))
    seq = lambda w: pl.BlockSpec((None, t, w), lambda n, b: (n, 0, 0))
    return pl.pallas_call(
        functools.partial(_attn_prompt_kernel, topk=topk),
        grid=(nb, t // tq),
        in_specs=[blk(N_IDX_HEADS * D_IDX),
                  pl.BlockSpec((None, N_IDX_HEADS, tq), lambda n, b: (n, 0, b)),
                  seq(LANES), seq(LANES), blk(aw), seq(aw),
                  pl.BlockSpec((None, aw, t), lambda n, b: (n, 0, 0))],
        out_specs=blk(aw),
        out_shape=jax.ShapeDtypeStruct((nb, t, aw), BF16),
        scratch_shapes=[pltpu.VMEM((t, tq), F32),
                        pltpu.VMEM((N_HEADS // 2, 2 * tq, LANES), BF16),
                        pltpu.VMEM((N_HEADS // 2, 2 * tq, 2 * HEAD_DIM), BF16),
                        pltpu.VMEM((N_HEADS, tq), F32), pltpu.VMEM((N_HEADS, tq), F32),
                        pltpu.VMEM((aw, tq), F32), pltpu.VMEM((1, tq), F32),
                        pltpu.VMEM((N_HEADS // 2, ATTN_KC, 2 * tq), F32),
                        pltpu.VMEM((N_HEADS // 2, ATTN_KC, 2 * tq), F32)],
        compiler_params=_params("parallel", "arbitrary"),
        name="attn_prompt",
    )(qi, wit, kie, kio, q, k, vt)


IDX_PAGES_PER_STEP = 16


def _idx_pages_kernel(pt_ref, *refs):
    g_pages = IDX_PAGES_PER_STEP
    page_refs = refs[:g_pages]
    qi_ref, wi_ref, kinew_ref, o_ref, onew_ref = refs[g_pages:]
    wi = wi_ref[...]
    qi = qi_ref[...]
    for g in range(g_pages):
        d = _dot(qi, page_refs[g][...].astype(BF16))
        o_ref[g:g + 1, :] = jnp.sum(wi * jnp.maximum(d, 0.0), axis=0, keepdims=True)
    d_new = jnp.sum(qi.astype(F32) * kinew_ref[...].astype(F32), axis=1, keepdims=True)
    onew_ref[...] = jnp.sum(wi * jnp.maximum(d_new, 0.0), axis=0, keepdims=True)


def _idx_scores_sample(page_table, cache_kidx_t, qi, wi, ki_new):
    nb, n_pages = page_table.shape
    g_pages = IDX_PAGES_PER_STEP
    page_specs = [pl.BlockSpec((None, None, D_IDX, PAGE_SIZE),
                               lambda n, p, pt, g=g: (0, pt[n, p * g_pages + g], 0, 0))
                  for g in range(g_pages)]
    return pl.pallas_call(
        _idx_pages_kernel,
        grid_spec=pltpu.PrefetchScalarGridSpec(
            num_scalar_prefetch=1,
            grid=(nb, n_pages // g_pages),
            in_specs=page_specs + [
                pl.BlockSpec((None, N_IDX_HEADS, D_IDX), lambda n, p, pt: (n, 0, 0)),
                pl.BlockSpec((None, N_IDX_HEADS, 1), lambda n, p, pt: (n, 0, 0)),
                pl.BlockSpec((None, 1, D_IDX), lambda n, p, pt: (n, 0, 0))],
            out_specs=[pl.BlockSpec((None, g_pages, PAGE_SIZE), lambda n, p, pt: (n, p, 0)),
                       pl.BlockSpec((None, 1, 1), lambda n, p, pt: (n, 0, 0))]),
        out_shape=[jax.ShapeDtypeStruct((nb, n_pages, PAGE_SIZE), F32),
                   jax.ShapeDtypeStruct((nb, 1, 1), F32)],
        compiler_params=_params("parallel", "arbitrary"),
        name="idx_scores_sample",
    )(page_table, *([cache_kidx_t] * g_pages), qi, wi, ki_new)


def _select_sample_kernel(sc_ref, snew_ref, sel_ref, selnew_ref, *, topk):
    scores = sc_ref[...]
    s_new = snew_ref[...]

    def count_ge(t):
        hit = jnp.where(scores >= t, 1.0, 0.0)
        return jnp.sum(hit, axis=1, keepdims=True) + jnp.where(s_new >= t, 1.0, 0.0)

    thr = _kth_largest(count_ge, float(topk), (scores.shape[0], 1))
    sel_ref[...] = jnp.where(scores >= thr, 1.0, 0.0)
    selnew_ref[...] = jnp.where(s_new >= thr, 1, 0).astype(I32)


def _select_sample(scores, s_new, topk):
    nb, past = scores.shape
    return pl.pallas_call(
        functools.partial(_select_sample_kernel, topk=topk),
        out_shape=[jax.ShapeDtypeStruct((nb, past), F32), jax.ShapeDtypeStruct((nb, 1), I32)],
        compiler_params=pltpu.CompilerParams(vmem_limit_bytes=VMEM_LIMIT),
        name="select_sample",
    )(scores, s_new)


def _compact_kernel(sel_ref, code_ref, *, n_slots):
    sel = sel_ref[...]
    n_pages = sel.shape[0]
    sel_bf = sel.astype(BF16)
    lane_r = lax.broadcasted_iota(I32, (PAGE_SIZE, PAGE_SIZE), 0)
    lane_c = lax.broadcasted_iota(I32, (PAGE_SIZE, PAGE_SIZE), 1)
    before = jnp.where(lane_r < lane_c, 1.0, 0.0).astype(BF16)
    rank_in_page = _dot(sel_bf, before)
    ones = jnp.ones((SUBLANES, PAGE_SIZE), BF16)
    cnt_row = _dot_nt(ones, sel_bf)[0:1, :]
    pg_r = lax.broadcasted_iota(I32, (n_pages, n_pages), 0)
    pg_c = lax.broadcasted_iota(I32, (n_pages, n_pages), 1)
    upto = jnp.where(pg_r <= pg_c, 1.0, 0.0).astype(BF16)
    cnt8 = jnp.broadcast_to(cnt_row, (SUBLANES, n_pages)).astype(BF16)
    incl_row = _dot(cnt8, upto)[0:1, :]
    excl_row = incl_row - cnt_row

    slot = lax.broadcasted_iota(I32, (n_slots, 1), 0).astype(F32)
    in_page = jnp.where(excl_row <= slot, jnp.where(slot < incl_row, 1.0, 0.0), 0.0)
    page_ids = lax.broadcasted_iota(I32, (1, n_pages), 1).astype(F32)
    page_of_slot = jnp.sum(in_page * page_ids, axis=1, keepdims=True)
    rank_of_slot = slot - jnp.sum(in_page * excl_row, axis=1, keepdims=True)
    ranks = jnp.where(sel > 0.5, rank_in_page, -1.0).astype(BF16)
    page_ranks = _dot(in_page.astype(BF16), ranks)
    lane_ids = lax.broadcasted_iota(I32, (1, PAGE_SIZE), 1).astype(F32)
    off_of_slot = jnp.sum(jnp.where(page_ranks == rank_of_slot, lane_ids, 0.0), axis=1, keepdims=True)
    used = jnp.sum(in_page, axis=1, keepdims=True)
    code = jnp.where(used > 0.5, page_of_slot * PAGE_SIZE + off_of_slot, -1.0)
    code_ref[...] = code.astype(I32)


def _compact_sample(sel, n_slots):
    nb, n_pages, _ = sel.shape
    return pl.pallas_call(
        functools.partial(_compact_kernel, n_slots=n_slots),
        grid=(nb,),
        in_specs=[pl.BlockSpec((None, n_pages, PAGE_SIZE), lambda n: (n, 0, 0))],
        out_specs=pl.BlockSpec((None, n_slots, 1), lambda n: (n, 0, 0)),
        out_shape=jax.ShapeDtypeStruct((nb, n_slots, 1), I32),
        compiler_params=_params("parallel"),
        name="compact_sample",
    )(sel)


def _attn_sample_kernel(pt_ref, code_ref, selnew_ref, ck_ref, cv_ref, q_ref, kn_ref, vn_ref, o_ref,
                        kbuf, vbuf, ksem, vsem):
    n = pl.program_id(0)
    n_seq = pl.num_programs(0)
    n_slots = kbuf.shape[1]

    def start_rows(seq, slot):
        def body(s, carry):
            code = jnp.maximum(code_ref[seq, s], 0)
            phys = pt_ref[seq, lax.shift_right_logical(code, jnp.int32(PAGE_SHIFT))]
            off = code & (PAGE_SIZE - 1)
            pltpu.make_async_copy(ck_ref.at[0, phys, off], kbuf.at[slot, s], ksem.at[slot]).start()
            pltpu.make_async_copy(cv_ref.at[0, phys, off], vbuf.at[slot, s], vsem.at[slot]).start()
            return carry
        lax.fori_loop(0, n_slots, body, 0, unroll=8)

    def wait_rows(seq, slot):
        pltpu.make_async_copy(kbuf.at[slot], kbuf.at[slot], ksem.at[slot]).wait()
        pltpu.make_async_copy(vbuf.at[slot], vbuf.at[slot], vsem.at[slot]).wait()

    cur = n % 2

    @pl.when(n == 0)
    def _():
        start_rows(0, 0)

    @pl.when(n + 1 < n_seq)
    def _():
        start_rows(n + 1, 1 - cur)

    wait_rows(n, cur)

    n_used = jnp.where(code_ref[n, n_slots - 1] < 0, n_slots - 1, n_slots)
    new_on = selnew_ref[n] > 0
    q = q_ref[...]
    rnd = lambda a: a.astype(BF16).astype(F32)
    s = jnp.sum(rnd(kbuf[cur]) * q[None], axis=-1, keepdims=True)
    live = lax.broadcasted_iota(I32, s.shape, 0) < n_used
    s = jnp.where(live, s, MASKED_SCORE)
    s_new = jnp.sum(rnd(kn_ref[...]) * q, axis=-1, keepdims=True)
    s_new = jnp.where(new_on, s_new, MASKED_SCORE)
    m = jnp.maximum(jnp.max(s, axis=0), s_new)
    p = jnp.where(live, jnp.exp(s - m[None]), 0.0)
    p_new = jnp.where(new_on, jnp.exp(s_new - m), 0.0)
    l = jnp.sum(p, axis=0) + p_new
    acc = jnp.sum(rnd(p) * rnd(vbuf[cur]), axis=0) + rnd(p_new) * rnd(vn_ref[...])
    o_ref[...] = acc / l


def _attn_sample(page_table, codes, sel_new, cache_k, cache_v, q, k_new, v_new):
    nb, n_slots = codes.shape
    per_seq = pl.BlockSpec((None, N_HEADS, HEAD_DIM), lambda n, *_: (n, 0, 0))
    hbm = pl.BlockSpec(memory_space=pl.ANY)
    return pl.pallas_call(
        _attn_sample_kernel,
        grid_spec=pltpu.PrefetchScalarGridSpec(
            num_scalar_prefetch=3,
            grid=(nb,),
            in_specs=[hbm, hbm, per_seq, per_seq, per_seq],
            out_specs=per_seq,
            scratch_shapes=[pltpu.VMEM((2, n_slots, N_HEADS, HEAD_DIM), F32),
                            pltpu.VMEM((2, n_slots, N_HEADS, HEAD_DIM), F32),
                            pltpu.SemaphoreType.DMA((2,)), pltpu.SemaphoreType.DMA((2,))]),
        out_shape=jax.ShapeDtypeStruct((nb, N_HEADS, HEAD_DIM), F32),
        compiler_params=_params("arbitrary"),
        name="attn_sample",
    )(page_table, codes, sel_new, cache_k, cache_v, q, k_new, v_new)


def _outproj_kernel(a_ref, b_ref, w_ref, x_ref, gate_ref, o_ref, w_sc):
    @pl.when(pl.program_id(1) == 0)
    def _():
        w_sc[...] = w_ref[...].astype(BF16)

    c = a_ref.shape[1]
    mix = _dot(a_ref[...], w_sc[:c, :]) + _dot(b_ref[...], w_sc[c:, :])
    o_ref[...] = x_ref[...] + gate_ref[...] * mix


def _outproj(a, b, w_out, x, gate, per_row, tm, tiles_per_batch):
    m, c = a.shape
    d = x.shape[1]
    tn = 1024
    if per_row:
        gate_spec = pl.BlockSpec((tm, tn), lambda j, i: (i, j))
    else:
        gate_spec = pl.BlockSpec((None, 1, tn), lambda j, i: (i // tiles_per_batch, 0, j))
    return pl.pallas_call(
        _outproj_kernel,
        grid=(d // tn, m // tm),
        in_specs=[pl.BlockSpec((tm, c), lambda j, i: (i, 0)),
                  pl.BlockSpec((tm, b.shape[1]), lambda j, i: (i, 0)),
                  pl.BlockSpec((w_out.shape[0], tn), lambda j, i: (0, j)),
                  pl.BlockSpec((tm, tn), lambda j, i: (i, j)),
                  gate_spec],
        out_specs=pl.BlockSpec((tm, tn), lambda j, i: (i, j)),
        out_shape=jax.ShapeDtypeStruct((m, d), F32),
        scratch_shapes=[pltpu.VMEM((w_out.shape[0], tn), BF16)],
        compiler_params=_params("arbitrary", "arbitrary"),
        name="outproj",
    )(a, b, w_out, x, gate)


def _mlp_kernel(h_ref, w1_ref, w2_ref, x_ref, gate_ref, fg_ref, o_ref):
    j = pl.program_id(1)
    @pl.when(j == 0)
    def _():
        o_ref[...] = jnp.zeros(o_ref.shape, F32)

    hid = _dot(h_ref[...], w1_ref[...])
    hid = jnp.square(jnp.maximum(hid, 0.0)).astype(BF16)
    o_ref[...] += _dot(hid, w2_ref[...])

    @pl.when(j == pl.num_programs(1) - 1)
    def _():
        x = x_ref[...] + gate_ref[...] * o_ref[...]
        y = x * lax.rsqrt(jnp.mean(x * x, axis=-1, keepdims=True) + EPS)
        o_ref[...] = y * fg_ref[...]


def _mlp(h, w1, w2, x, gate, final_g, per_row, tm, tiles_per_batch):
    m, d = h.shape
    f = w1.shape[1]
    tf = min(1024, f)
    ms = _mod_spec(per_row, tm, tiles_per_batch, d)
    return pl.pallas_call(
        _mlp_kernel,
        grid=(m // tm, f // tf),
        in_specs=[pl.BlockSpec((tm, d), lambda i, j: (i, 0)),
                  pl.BlockSpec((d, tf), lambda i, j: (0, j)),
                  pl.BlockSpec((tf, d), lambda i, j: (j, 0)),
                  pl.BlockSpec((tm, d), lambda i, j: (i, 0)),
                  ms,
                  pl.BlockSpec((1, d), lambda i, j: (0, 0))],
        out_specs=pl.BlockSpec((tm, d), lambda i, j: (i, 0)),
        out_shape=jax.ShapeDtypeStruct((m, d), F32),
        compiler_params=_params("parallel", "arbitrary"),
        name="mlp",
    )(h, w1, w2, x, gate, final_g.reshape(1, d))


def _split_mod(mod, per_row):
    parts = jnp.split(mod, 6, axis=-1)
    return parts if per_row else [p[:, None, :] for p in parts]


def _trunk(x, nb, mod, per_row, pos, tm, weights, conv_fn, attend_fn):
    (norm1_g, w_in_t, w_out, norm2_g, w1_bf, w2_bf, final_g) = weights
    m, d = x.shape
    c_conv = d // 2
    attn_w = N_HEADS * HEAD_DIM
    shift1, scale1, gate1, shift2, scale2, gate2 = _split_mod(mod, per_row)
    tiles_per_seq = pos.shape[0] // tm
    tiles_per_batch = (m // nb) // tm if not per_row else 1

    h = _prenorm(x, norm1_g, shift1, scale1, per_row, tm, tiles_per_batch)
    u = _proj_glu(h, w_in_t, c_conv, tm)
    cos_h, sin_h = _rope_tables(pos, HEAD_DIM)
    cos_i, sin_i = _rope_tables(pos, D_IDX)
    seg = functools.partial(_proj_seg, h, w_in_t, nb=nb, tiles_per_seq=tiles_per_seq, tm=tm)
    col_q = 2 * c_conv
    (q_bf,) = seg(col_q, attn_w, cos_h, sin_h, head_w=HEAD_DIM, out_scale=HEAD_DIM ** -0.5,
                  outs=("bf16",), name="proj_q")
    k_f32, k_bf = seg(col_q + attn_w, attn_w, cos_h, sin_h, head_w=HEAD_DIM,
                      outs=("f32", "bf16"), name="proj_k")
    v_outs = seg(col_q + 2 * attn_w, attn_w, cos_h, sin_h, head_w=0,
                 outs=("f32",) if per_row else ("f32", "bf16_t"), name="proj_v")
    (qi_bf,) = seg(col_q + 3 * attn_w, N_IDX_HEADS * D_IDX, cos_i, sin_i, head_w=D_IDX,
                   outs=("bf16",), name="proj_qi")
    col_ki = col_q + 3 * attn_w + N_IDX_HEADS * D_IDX
    w_kiwi = jnp.pad(w_in_t[col_ki:], ((0, LANES - (w_in_t.shape[0] - col_ki)), (0, 0)))
    kiwi = _proj_kiwi(h, w_kiwi, cos_i, sin_i, nb, tiles_per_seq, tm,
                      (D_IDX ** -0.5) * (N_IDX_HEADS ** -0.5), want_t=not per_row)

    a_out = conv_fn(u)
    b_out = attend_fn(q_bf, k_f32, k_bf, v_outs, qi_bf, kiwi)
    x1 = _outproj(a_out, b_out, w_out, x, gate1, per_row, tm, tiles_per_batch)
    h2 = _prenorm(x1, norm2_g, shift2, scale2, per_row, tm, tiles_per_batch)
    y = _mlp(h2, w1_bf, w2_bf, x1, gate2, final_g, per_row, tm, tiles_per_batch)
    return y, k_f32, v_outs[0], kiwi[0], u


def kernel(x_prompt, x_sample, cache_k, cache_v, cache_kidx, state_conv, page_table, c_prompt, c_sample,
           norm1_g, w_ada, b_ada, w_in, conv_w, conv_b, conv_ln_g, conv_ln_b, w_out, norm2_g,
           w_ff1, w_ff2, final_g):
    nb, t, d = x_prompt.shape
    ns, ts, _ = x_sample.shape
    assert ts == 1 and norm1_g.shape[0] == 1, "one new token per sequence, depth one"
    c_conv = d // 2
    attn_w = N_HEADS * HEAD_DIM
    n_pages = page_table.shape[1]
    past = n_pages * PAGE_SIZE

    rows = nb + ns
    pad = (-rows) % SUBLANES
    c_all = jnp.pad(jnp.concatenate([c_prompt, c_sample], axis=0), ((0, pad), (0, 0)))
    mod = _ada(c_all, w_ada[0], b_ada[0])
    weights = (norm1_g[0], jnp.swapaxes(w_in[0], 0, 1), w_out[0], norm2_g[0],
               w_ff1[0].astype(BF16), w_ff2[0].astype(BF16), final_g)
    conv_params = (conv_w[0], conv_b[0], conv_ln_g[0], conv_ln_b[0])

    tm_p = 512
    topk_p = min(TOPK_MAX, t // 4)

    def conv_p(u):
        return _conv_prompt(u.reshape(nb, t, c_conv), *conv_params, 256).reshape(nb * t, c_conv)

    def attend_p(q_bf, k_f32, k_bf, v_outs, qi_bf, kiwi):
        _, kie, kio, _, wit = kiwi
        r3 = lambda a: a.reshape(nb, t, a.shape[-1])
        o = _attn_prompt(r3(qi_bf), wit, r3(kie), r3(kio), r3(q_bf), r3(k_bf), v_outs[1], topk_p)
        return o.reshape(nb * t, attn_w)

    yp, kp, vp, kip, up = _trunk(x_prompt.reshape(nb * t, d), nb, mod[:nb], False, jnp.arange(t), tm_p,
                                 weights, conv_p, attend_p)

    topk_s = min(TOPK_MAX, (past + 1) // 4)
    state_t = jnp.swapaxes(state_conv[0], 0, 1)

    def conv_s(u):
        return _conv_sample(state_t, u, *conv_params)

    def attend_s(q_bf, k_f32, k_bf, v_outs, qi_bf, kiwi):
        _, kie, _, wi = kiwi
        qi3 = qi_bf.reshape(ns, N_IDX_HEADS, D_IDX)
        scores, s_new = _idx_scores_sample(page_table, jnp.swapaxes(cache_kidx, 2, 3), qi3, wi[:, :, None],
                                           kie[:, None, :D_IDX])
        sel, sel_new = _select_sample(scores.reshape(ns, past), s_new.reshape(ns, 1), topk_s)
        codes = _compact_sample(sel.reshape(ns, n_pages, PAGE_SIZE), topk_s)
        per_head = lambda a: a.astype(F32).reshape(ns, N_HEADS, HEAD_DIM)
        o = _attn_sample(page_table, codes.reshape(ns, topk_s), sel_new.reshape(ns), cache_k, cache_v,
                         per_head(q_bf), per_head(k_f32), per_head(v_outs[0]))
        return o.reshape(ns, attn_w).astype(BF16)

    ys, ks, vs, kis, us = _trunk(x_sample.reshape(ns, d), ns, mod[nb:nb + ns], True,
                                 jnp.full((ns,), past, jnp.int32), ns, weights, conv_s, attend_s)

    heads = lambda a, n, tt: a.reshape(1, n, tt, N_HEADS, HEAD_DIM)
    conv_prompt = up.reshape(nb, t, c_conv)[:, t - (CONV_W - 1):][None]
    conv_sample = jnp.concatenate([state_conv[0][:, 1:], us[:, None, :]], axis=1)[None]
    return (yp.reshape(nb, t, d), ys.reshape(ns, 1, d),
            heads(kp, nb, t), heads(vp, nb, t), kip.reshape(1, nb, t, D_IDX), conv_prompt,
            heads(ks, ns, 1), heads(vs, ns, 1), kis.reshape(1, ns, 1, D_IDX), conv_sample)
```
